```python
import jax
import jax.numpy as jnp
from jax import lax
import numpy as np

D_MODEL = 1024
BATCH = 2
SEQ = 8192
DEPTH = 2
DEC_BATCH = 128
DEC_SEQ = 8
PAST_LEN = 8192
PAGE_SIZE = 128

N_MIXERS = 2
N_A_LAYERS = (DEPTH + 1) // 2
N_B_LAYERS = DEPTH // 2
A_GROUPS = ((128, 1), (512, 4), (2048, 16))
A_N_GROUPS = len(A_GROUPS)
A_HEADS = 8
A_HEAD_DIM = D_MODEL // A_HEADS
A_WIDTH = A_HEADS * A_HEAD_DIM
B_WINDOW = 128
B_Q_HEADS = 16
B_KV_HEADS = 4
B_GROUP = B_Q_HEADS // B_KV_HEADS
B_HEAD_DIM = 64
B_Q_WIDTH = B_Q_HEADS * B_HEAD_DIM
B_KV_WIDTH = 2 * B_KV_HEADS * B_HEAD_DIM
BLK = 128
D_FF = -(-8 * D_MODEL // (3 * 256)) * 256
RMS_EPS = 1e-6
NEG_INF = -1e30

kernel_name = "hybrid_dilated_swa_sink_adaln_decoder_step"


def rms_norm(x, g):
    xf = x.astype(jnp.float32)
    y = xf * lax.rsqrt(jnp.mean(xf * xf, axis=-1, keepdims=True) + RMS_EPS)
    return (y * g.astype(jnp.float32)).astype(x.dtype)


def alibi_slopes(n):
    return 2.0 ** (-8.0 * jnp.arange(1, n + 1, dtype=jnp.float32) / n)


def a_slopes():
    return alibi_slopes(A_N_GROUPS * A_HEADS).reshape(A_N_GROUPS, A_HEADS)


def b_slopes():
    return alibi_slopes(B_Q_HEADS).reshape(B_KV_HEADS, B_GROUP)


def swiglu(h, w_gate, w_up, w_down):
    return (jax.nn.silu(h @ w_gate) * (h @ w_up)) @ w_down


def softmax_stats(s, sink):
    m = s.max(-1)
    if sink is not None:
        m = jnp.maximum(m, sink)
    p = jnp.exp(s - m[..., None])
    l = p.sum(-1)
    if sink is not None:
        l = l + jnp.exp(sink - m)
    return p, l, m + jnp.log(l)


def banded_attention(q, k, v, slopes, step, window, sinks):
    n, s_len, hk, g, dh = q.shape
    nb = s_len // BLK
    qb = q.reshape(n, nb, BLK, hk, g, dh)

    def band(t):
        tb = t.reshape(n, nb, BLK, hk, dh)
        prev = jnp.concatenate([jnp.zeros_like(tb[:, :1]), tb[:, :-1]], axis=1)
        return jnp.concatenate([prev, tb], axis=2)

    kb, vb = band(k), band(v)
    s = jnp.einsum('bcqhgd,bckhd->bchgqk', qb, kb, preferred_element_type=jnp.float32) * (dh ** -0.5)
    dist = (jnp.arange(BLK)[:, None] + BLK) - jnp.arange(2 * BLK)[None, :]
    key_row = jnp.arange(nb)[:, None] * BLK - BLK + jnp.arange(2 * BLK)[None, :]
    valid = ((dist >= 0) & (dist <= window))[None] & (key_row >= 0)[:, None, :]
    bias = -(slopes.astype(jnp.float32) * step)[:, :, None, None] * dist.astype(jnp.float32)
    s = jnp.where(valid[None, :, None, None], s + bias, NEG_INF)
    sink = None if sinks is None else sinks.astype(jnp.float32)[:, :, None]
    p, l, lse = softmax_stats(s, sink)
    o = jnp.einsum('bchgqk,bckhd->bcqhgd', p, vb, preferred_element_type=jnp.float32)
    o = o / jnp.moveaxis(l, -1, 2)[..., None]
    return o.reshape(n, s_len, hk, g, dh), jnp.moveaxis(lse, -1, 2).reshape(n, s_len, hk, g)


def gathered_attention(q, kv_cat, slopes, step, window, sinks):
    n, t, hk, g, dh = q.shape
    past = kv_cat.shape[1] - t
    j = jnp.arange(window // step + 1)
    idx = past + jnp.arange(t)[:, None] - j[None, :] * step
    kvg = jnp.take(kv_cat, jnp.maximum(idx, 0), axis=1)
    s = jnp.einsum('bthgd,btjhd->bthgj', q, kvg[:, :, :, 0], preferred_element_type=jnp.float32) * (dh ** -0.5)
    bias = -(slopes.astype(jnp.float32) * step)[:, :, None] * j.astype(jnp.float32)
    s = jnp.where((idx >= 0)[None, :, None, None, :], s + bias, NEG_INF)
    sink = None if sinks is None else sinks.astype(jnp.float32)
    p, l, lse = softmax_stats(s, sink)
    o = jnp.einsum('bthgj,btjhd->bthgd', p, kvg[:, :, :, 1], preferred_element_type=jnp.float32)
    return o / l[..., None], lse


def dilated_prompt(q, k, v, slopes, window, dil):
    b, s_len, hh, dh = q.shape
    n_sub = s_len // dil
    n_pad = -(-n_sub // BLK) * BLK

    def to_sub(t):
        t = t.reshape(b, n_sub, dil, hh, dh).transpose(0, 2, 1, 3, 4).reshape(b * dil, n_sub, hh, dh)
        return jnp.pad(t, ((0, 0), (0, n_pad - n_sub), (0, 0), (0, 0)))

    o, lse = banded_attention(to_sub(q)[:, :, :, None], to_sub(k), to_sub(v), slopes[:, None], dil, window // dil, None)
    o = o[:, :n_sub, :, 0].reshape(b, dil, n_sub, hh, dh).transpose(0, 2, 1, 3, 4).reshape(b, s_len, hh, dh)
    lse = lse[:, :n_sub, :, 0].reshape(b, dil, n_sub, hh).transpose(0, 2, 1, 3).reshape(b, s_len, hh)
    return o, lse


def combine_by_denominator(outs, lses):
    w = jax.nn.softmax(jnp.stack(lses), axis=0)
    return jnp.einsum('gbsh,gbshd->bshd', w, jnp.stack(outs))


def mixer_a_prompt(h, w_in, w_out):
    b, s_len, _ = h.shape
    qkv = (h @ w_in).reshape(b, s_len, A_N_GROUPS, 3, A_HEADS, A_HEAD_DIM)
    slopes = a_slopes()
    outs, lses, rows = [], [], []
    for g, (window, dil) in enumerate(A_GROUPS):
        o, lse = dilated_prompt(qkv[:, :, g, 0], qkv[:, :, g, 1], qkv[:, :, g, 2], slopes[g], window, dil)
        outs.append(o)
        lses.append(lse)
        keep = min(window, s_len)
        rows.append(qkv[:, s_len - keep:, g, 1:3])
    o = combine_by_denominator(outs, lses).reshape(b, s_len, A_WIDTH).astype(h.dtype)
    return o @ w_out, rows


def mixer_a_sample(h, caches, w_in, w_out):
    b, t, _ = h.shape
    qkv = (h @ w_in).reshape(b, t, A_N_GROUPS, 3, A_HEADS, A_HEAD_DIM)
    slopes = a_slopes()
    outs, lses, rows = [], [], []
    for g, (window, dil) in enumerate(A_GROUPS):
        kv_new = qkv[:, :, g, 1:3]
        kv_cat = jnp.concatenate([caches[g], kv_new.astype(caches[g].dtype)], axis=1)
        o, lse = gathered_attention(qkv[:, :, g, 0][:, :, :, None], kv_cat, slopes[g][:, None], dil, window, None)
        outs.append(o[:, :, :, 0])
        lses.append(lse[:, :, :, 0])
        rows.append(kv_new)
    o = combine_by_denominator(outs, lses).reshape(b, t, A_WIDTH).astype(h.dtype)
    return o @ w_out, rows


def mixer_b_prompt(h, w_in, sinks, w_out):
    b, s_len, _ = h.shape
    proj = h @ w_in
    q = proj[..., :B_Q_WIDTH].reshape(b, s_len, B_KV_HEADS, B_GROUP, B_HEAD_DIM)
    kv = proj[..., B_Q_WIDTH:].reshape(b, s_len, 2, B_KV_HEADS, B_HEAD_DIM)
    o, _ = banded_attention(q, kv[:, :, 0], kv[:, :, 1], b_slopes(), 1, B_WINDOW, sinks)
    keep = min(B_WINDOW, s_len)
    return o.reshape(b, s_len, B_Q_WIDTH).astype(h.dtype) @ w_out, kv[:, s_len - keep:]


def mixer_b_sample(h, cache, w_in, sinks, w_out):
    b, t, _ = h.shape
    proj = h @ w_in
    q = proj[..., :B_Q_WIDTH].reshape(b, t, B_KV_HEADS, B_GROUP, B_HEAD_DIM)
    kv_new = proj[..., B_Q_WIDTH:].reshape(b, t, 2, B_KV_HEADS, B_HEAD_DIM)
    kv_cat = jnp.concatenate([cache, kv_new.astype(cache.dtype)], axis=1)
    o, _ = gathered_attention(q, kv_cat, b_slopes(), 1, B_WINDOW, sinks)
    return o.reshape(b, t, B_Q_WIDTH).astype(h.dtype) @ w_out, kv_new


def setup_inputs(seed: int = 0) -> dict:
    key = jax.random.key(seed)
    ks = jax.random.split(key, 24)

    def nrm(k, shape, scale=1.0):
        return jax.random.normal(k, shape, jnp.float32) * scale

    d = D_MODEL
    inp = {}
    inp['x_prompt'] = nrm(ks[0], (BATCH, SEQ, d))
    inp['x_sample'] = nrm(ks[1], (DEC_BATCH, DEC_SEQ, d))
    for gi, (window, _) in enumerate(A_GROUPS):
        inp['cache_a_w%d' % window] = nrm(ks[2 + gi], (N_A_LAYERS, DEC_BATCH, min(window, PAST_LEN), 2, A_HEADS, A_HEAD_DIM))
    inp['cache_b'] = nrm(ks[5], (N_B_LAYERS, DEC_BATCH, min(B_WINDOW, PAST_LEN), 2, B_KV_HEADS, B_HEAD_DIM))
    inp['c_prompt'] = nrm(ks[6], (BATCH, d))
    inp['c_sample'] = nrm(ks[7], (DEC_BATCH, d))
    inp['norm_mix_g'] = 1.0 + nrm(ks[8], (DEPTH, d), 0.1)
    inp['norm_ffn_g'] = 1.0 + nrm(ks[9], (DEPTH, d), 0.1)
    inp['w_ada'] = nrm(ks[10], (DEPTH, d, 6 * d), 0.5 * d ** -0.5)
    inp['b_ada'] = nrm(ks[11], (DEPTH, 6 * d), 0.02)
    inp['w_a_in'] = nrm(ks[12], (N_A_LAYERS, d, A_N_GROUPS * 3 * A_WIDTH), d ** -0.5)
    inp['w_a_out'] = nrm(ks[13], (N_A_LAYERS, A_WIDTH, d), A_WIDTH ** -0.5)
    inp['w_b_in'] = nrm(ks[14], (N_B_LAYERS, d, B_Q_WIDTH + B_KV_WIDTH), d ** -0.5)
    inp['b_sinks'] = nrm(ks[15], (N_B_LAYERS, B_KV_HEADS, B_GROUP), 0.5)
    inp['w_b_out'] = nrm(ks[16], (N_B_LAYERS, B_Q_WIDTH, d), B_Q_WIDTH ** -0.5)
    inp['w_ffn_gate'] = nrm(ks[17], (DEPTH, d, D_FF), d ** -0.5)
    inp['w_ffn_up'] = nrm(ks[18], (DEPTH, d, D_FF), d ** -0.5)
    inp['w_ffn_down'] = nrm(ks[19], (DEPTH, D_FF, d), D_FF ** -0.5)
    inp['norm_final_g'] = 1.0 + nrm(ks[20], (d,), 0.1)
    return inp


def reference(x_prompt, x_sample, cache_a_w128, cache_a_w512, cache_a_w2048, cache_b, c_prompt, c_sample,
              norm_mix_g, norm_ffn_g, w_ada, b_ada, w_a_in, w_a_out, w_b_in, b_sinks, w_b_out,
              w_ffn_gate, w_ffn_up, w_ffn_down, norm_final_g):
    a_caches = (cache_a_w128, cache_a_w512, cache_a_w2048)

    def trunk(x, c, sample):
        a_rows = [[] for _ in A_GROUPS]
        b_rows = []
        for i in range(DEPTH):
            li = i // N_MIXERS
            mod = (jax.nn.silu(c) @ w_ada[i] + b_ada[i])[:, None, :]
            shift1, scale1, gate1, shift2, scale2, gate2 = jnp.split(mod, 6, axis=-1)
            h = rms_norm(x, norm_mix_g[i]) * (1 + scale1) + shift1
            if i % N_MIXERS == 0:
                if sample:
                    y, rows = mixer_a_sample(h, [cc[li] for cc in a_caches], w_a_in[li], w_a_out[li])
                else:
                    y, rows = mixer_a_prompt(h, w_a_in[li], w_a_out[li])
                for g in range(A_N_GROUPS):
                    a_rows[g].append(rows[g])
            else:
                if sample:
                    y, rows = mixer_b_sample(h, cache_b[li], w_b_in[li], b_sinks[li], w_b_out[li])
                else:
                    y, rows = mixer_b_prompt(h, w_b_in[li], b_sinks[li], w_b_out[li])
                b_rows.append(rows)
            x = x + gate1 * y
            h = rms_norm(x, norm_ffn_g[i]) * (1 + scale2) + shift2
            x = x + gate2 * swiglu(h, w_ffn_gate[i], w_ffn_up[i], w_ffn_down[i])
        return rms_norm(x, norm_final_g), [jnp.stack(r) for r in a_rows], jnp.stack(b_rows)

    y_prompt, (p_a128, p_a512, p_a2048), p_b = trunk(x_prompt, c_prompt, False)
    y_sample, (s_a128, s_a512, s_a2048), s_b = trunk(x_sample, c_sample, True)
    return (y_prompt, y_sample, p_a128, p_a512, p_a2048, p_b, s_a128, s_a512, s_a2048, s_b)
```

```python
import functools

import jax
import jax.numpy as jnp
from jax import lax
from jax.experimental import pallas as pl
from jax.experimental.pallas import tpu as pltpu

A_GROUPS = ((128, 1), (512, 4), (2048, 16))
A_N_GROUPS = len(A_GROUPS)
A_HEADS = 8
A_HEAD_DIM = 128
A_WIDTH = A_HEADS * A_HEAD_DIM
B_WINDOW = 128
B_Q_HEADS = 16
B_KV_HEADS = 4
B_GROUP = B_Q_HEADS // B_KV_HEADS
B_HEAD_DIM = 64
B_Q_WIDTH = B_Q_HEADS * B_HEAD_DIM
B_KV_HALF = B_KV_HEADS * B_HEAD_DIM
BLK = 128
RMS_EPS = 1e-6
NEG_INF = -1e30

LANES = 128
SUBLANES = 8
LSE_REP = LANES // A_HEADS
VMEM_LIMIT = 56 * 1024 * 1024

_f32 = jnp.float32
_bf16 = jnp.bfloat16
_NT = (((1,), (1,)), ((), ()))


def _params(*sem):
    return pltpu.CompilerParams(dimension_semantics=sem, vmem_limit_bytes=VMEM_LIMIT)


def _rms_mod(x, g, scale, shift):
    y = x * lax.rsqrt(jnp.mean(x * x, axis=-1, keepdims=True) + RMS_EPS) * g
    return y * (1.0 + scale) + shift


def _silu(x):
    return x * (1.0 / (1.0 + jnp.exp(-x)))


def _log2(n):
    assert n > 0 and n & (n - 1) == 0, n
    return n.bit_length() - 1


def _div(x, n):
    return lax.shift_right_arithmetic(x, jnp.int32(_log2(n)))


def _rem(x, n):
    return x & (n - 1)


def _mod_kernel(c_ref, w_ref, b_ref, o_ref):
    c = _silu(c_ref[...]).astype(_bf16)
    o_ref[0] = jnp.dot(c, w_ref[0].astype(_bf16), preferred_element_type=_f32) + b_ref[0]


def _modulation(c_all, w_ada, b_ada):
    depth, d, n = w_ada.shape
    nb = c_all.shape[0]
    tn = 1024
    return pl.pallas_call(
        _mod_kernel,
        out_shape=jax.ShapeDtypeStruct((depth, nb, n), _f32),
        grid=(depth, n // tn),
        in_specs=[
            pl.BlockSpec((nb, d), lambda l, j: (0, 0)),
            pl.BlockSpec((1, d, tn), lambda l, j: (l, 0, j)),
            pl.BlockSpec((1, 1, tn), lambda l, j: (l, 0, j)),
        ],
        out_specs=pl.BlockSpec((1, nb, tn), lambda l, j: (l, 0, j)),
        compiler_params=_params("parallel", "parallel"),
        name="adaln_modulation",
    )(c_all, w_ada, b_ada.reshape(depth, 1, n))


def _mod_spec(mod, col, tm, rows_per_batch, ngrid):
    d = mod.shape[-1] // 6
    if mod.ndim == 3:
        tiles = rows_per_batch // tm
        if ngrid == 1:
            return pl.BlockSpec((1, 1, d), lambda i: (i // tiles, 0, col))
        return pl.BlockSpec((1, 1, d), lambda i, j: (i // tiles, 0, col))
    if ngrid == 1:
        return pl.BlockSpec((tm, d), lambda i: (i, col))
    return pl.BlockSpec((tm, d), lambda i, j: (i, col))


def _mod_val(ref):
    return ref[0] if len(ref.shape) == 3 else ref[...]


def _norm_proj_kernel(x_ref, scale_ref, shift_ref, g_ref, w_ref, o_ref, h_ref):
    @pl.when(pl.program_id(1) == 0)
    def _():
        h = _rms_mod(x_ref[...], g_ref[...], _mod_val(scale_ref), _mod_val(shift_ref))
        h_ref[...] = h.astype(_bf16)

    o_ref[...] = jnp.dot(h_ref[...], w_ref[...], preferred_element_type=_f32)


def _norm_proj(x, mod, g, w, *, rows_per_batch, tm, tn, name):
    m, d = x.shape
    n = w.shape[1]
    return pl.pallas_call(
        _norm_proj_kernel,
        out_shape=jax.ShapeDtypeStruct((m, n), _f32),
        grid=(m // tm, n // tn),
        in_specs=[
            pl.BlockSpec((tm, d), lambda i, j: (i, 0)),
            _mod_spec(mod, 1, tm, rows_per_batch, 2),
            _mod_spec(mod, 0, tm, rows_per_batch, 2),
            pl.BlockSpec((1, d), lambda i, j: (0, 0)),
            pl.BlockSpec((d, tn), lambda i, j: (0, j)),
        ],
        out_specs=pl.BlockSpec((tm, tn), lambda i, j: (i, j)),
        scratch_shapes=[pltpu.VMEM((tm, d), _bf16)],
        compiler_params=_params("parallel", "arbitrary"),
        name=name,
    )(x, mod, mod, g.reshape(1, d), w)


def _a_out_kernel(o0_ref, o1_ref, o2_ref, l0_ref, l1_ref, l2_ref, w_ref, x_ref, gate_ref, out_ref):
    l0, l1, l2 = l0_ref[...], l1_ref[...], l2_ref[...]
    m = jnp.maximum(jnp.maximum(l0, l1), l2)
    e0, e1, e2 = jnp.exp(l0 - m), jnp.exp(l1 - m), jnp.exp(l2 - m)
    inv = 1.0 / (e0 + e1 + e2)
    w0, w1, w2 = e0 * inv, e1 * inv, e2 * inv
    tm = o0_ref.shape[0]
    parts = []
    for h in range(A_HEADS):
        hs = slice(h * A_HEAD_DIM, (h + 1) * A_HEAD_DIM)
        c = h * LSE_REP
        shape = (tm, A_HEAD_DIM)
        comb = (jnp.broadcast_to(w0[:, c:c + 1], shape) * o0_ref[:, hs]
                + jnp.broadcast_to(w1[:, c:c + 1], shape) * o1_ref[:, hs]
                + jnp.broadcast_to(w2[:, c:c + 1], shape) * o2_ref[:, hs])
        parts.append(comb.astype(_bf16))
    comb = jnp.concatenate(parts, axis=1)
    y = jnp.dot(comb, w_ref[...], preferred_element_type=_f32)
    out_ref[...] = x_ref[...] + _mod_val(gate_ref) * y


def _a_out_proj(outs, lses, w, x, mod, *, rows_per_batch, tm):
    m, d = x.shape
    row = lambda i: (i, 0)
    return pl.pallas_call(
        _a_out_kernel,
        out_shape=jax.ShapeDtypeStruct((m, d), _f32),
        grid=(m // tm,),
        in_specs=[pl.BlockSpec((tm, A_WIDTH), row)] * 3 + [pl.BlockSpec((tm, LANES), row)] * 3 + [
            pl.BlockSpec((A_WIDTH, d), lambda i: (0, 0)),
            pl.BlockSpec((tm, d), row),
            _mod_spec(mod, 2, tm, rows_per_batch, 1),
        ],
        out_specs=pl.BlockSpec((tm, d), row),
        compiler_params=_params("parallel"),
        name="a_combine_out_proj",
    )(*outs, *lses, w, x, mod)


def _out_kernel(o_ref, w_ref, x_ref, gate_ref, out_ref):
    y = jnp.dot(o_ref[...].astype(_bf16), w_ref[...], preferred_element_type=_f32)
    out_ref[...] = x_ref[...] + _mod_val(gate_ref) * y


def _out_proj(o, w, x, mod, *, rows_per_batch, tm, name):
    m, d = x.shape
    k = o.shape[1]
    row = lambda i: (i, 0)
    return pl.pallas_call(
        _out_kernel,
        out_shape=jax.ShapeDtypeStruct((m, d), _f32),
        grid=(m // tm,),
        in_specs=[
            pl.BlockSpec((tm, k), row),
            pl.BlockSpec((k, d), lambda i: (0, 0)),
            pl.BlockSpec((tm, d), row),
            _mod_spec(mod, 2, tm, rows_per_batch, 1),
        ],
        out_specs=pl.BlockSpec((tm, d), row),
        compiler_params=_params("parallel"),
        name=name,
    )(o, w, x, mod)


def _ffn_kernel(x_ref, scale_ref, shift_ref, gate_ref, g_ref, wg_ref, wu_ref, wd_ref, gf_ref,
                o_ref, h_ref, acc_ref, *, final_norm):
    f = pl.program_id(1)

    @pl.when(f == 0)
    def _():
        h = _rms_mod(x_ref[...], g_ref[...], _mod_val(scale_ref), _mod_val(shift_ref))
        h_ref[...] = h.astype(_bf16)
        acc_ref[...] = jnp.zeros_like(acc_ref)

    h = h_ref[...]
    a = jnp.dot(h, wg_ref[...], preferred_element_type=_f32)
    b = jnp.dot(h, wu_ref[...], preferred_element_type=_f32)
    t = (_silu(a) * b).astype(_bf16)
    acc_ref[...] += jnp.dot(t, wd_ref[...], preferred_element_type=_f32)

    @pl.when(f == pl.num_programs(1) - 1)
    def _():
        y = x_ref[...] + _mod_val(gate_ref) * acc_ref[...]
        if final_norm:
            y = y * lax.rsqrt(jnp.mean(y * y, axis=-1, keepdims=True) + RMS_EPS) * gf_ref[...]
        o_ref[...] = y


def _ffn(x, mod, g, wg, wu, wd, gf, *, rows_per_batch, tm, tf, final_norm, name):
    m, d = x.shape
    dff = wg.shape[1]
    return pl.pallas_call(
        functools.partial(_ffn_kernel, final_norm=final_norm),
        out_shape=jax.ShapeDtypeStruct((m, d), _f32),
        grid=(m // tm, dff // tf),
        in_specs=[
            pl.BlockSpec((tm, d), lambda i, j: (i, 0)),
            _mod_spec(mod, 4, tm, rows_per_batch, 2),
            _mod_spec(mod, 3, tm, rows_per_batch, 2),
            _mod_spec(mod, 5, tm, rows_per_batch, 2),
            pl.BlockSpec((1, d), lambda i, j: (0, 0)),
            pl.BlockSpec((d, tf), lambda i, j: (0, j)),
            pl.BlockSpec((d, tf), lambda i, j: (0, j)),
            pl.BlockSpec((tf, d), lambda i, j: (j, 0)),
            pl.BlockSpec((1, d), lambda i, j: (0, 0)),
        ],
        out_specs=pl.BlockSpec((tm, d), lambda i, j: (i, 0)),
        scratch_shapes=[pltpu.VMEM((tm, d), _bf16), pltpu.VMEM((tm, d), _f32)],
        compiler_params=_params("parallel", "arbitrary"),
        name=name,
    )(x, mod, mod, mod, g.reshape(1, d), wg, wu, wd, gf.reshape(1, d))


def _band_dist_valid(window, first_block):
    row = lax.broadcasted_iota(jnp.int32, (BLK, 2 * BLK), 0)
    col = lax.broadcasted_iota(jnp.int32, (BLK, 2 * BLK), 1)
    dist = row + BLK - col
    first_col = jnp.where(first_block, BLK, 0)
    valid = jnp.minimum(jnp.minimum(dist, window - dist), col - first_col) >= 0
    return dist.astype(_f32), valid


def _attn_a_kernel(nslope_ref, q_ref, kp_ref, kc_ref, vp_ref, vc_ref, o_ref, lse_ref, *, window):
    dist, valid = _band_dist_valid(window, pl.program_id(2) == 0)
    scale = A_HEAD_DIM ** -0.5
    lane_head = _div(lax.broadcasted_iota(jnp.int32, (BLK, LANES), 1), LSE_REP)
    lse_tile = jnp.zeros((BLK, LANES), _f32)
    for h in range(A_HEADS):
        hs = slice(h * A_HEAD_DIM, (h + 1) * A_HEAD_DIM)
        q = q_ref[0, :, hs].astype(_bf16)
        k = jnp.concatenate([kp_ref[0, :, hs], kc_ref[0, :, hs]], axis=0).astype(_bf16)
        v = jnp.concatenate([vp_ref[0, :, hs], vc_ref[0, :, hs]], axis=0).astype(_bf16)
        s = lax.dot_general(q, k, _NT, preferred_element_type=_f32) * scale
        s = jnp.where(valid, s + nslope_ref[h] * dist, NEG_INF)
        m = jnp.max(s, axis=-1, keepdims=True)
        p = jnp.exp(s - m)
        l = jnp.sum(p, axis=-1, keepdims=True)
        o = jnp.dot(p.astype(_bf16), v, preferred_element_type=_f32)
        o_ref[0, :, hs] = o / l
        lse_tile = jnp.where(lane_head == h, m + jnp.log(l), lse_tile)
    lse_ref[0] = lse_tile


def _attn_a_prompt(qkv, nslopes, g, batch, seq):
    window, dil = A_GROUPS[g]
    n_sub = seq // dil
    nblk = n_sub // BLK
    cols = qkv.shape[1] // A_WIDTH
    view = qkv.reshape(batch, n_sub, dil * qkv.shape[1])
    blk = (1, BLK, A_WIDTH)

    def cur(which):
        return pl.BlockSpec(blk, lambda b, r, j: (b, j, r * cols + g * 3 + which))

    def prev(which):
        return pl.BlockSpec(blk, lambda b, r, j: (b, jnp.maximum(j - 1, 0), r * cols + g * 3 + which))

    o, lse = pl.pallas_call(
        functools.partial(_attn_a_kernel, window=window // dil),
        out_shape=(jax.ShapeDtypeStruct((batch, n_sub, dil * A_WIDTH), _f32),
                   jax.ShapeDtypeStruct((batch, n_sub, dil * LANES), _f32)),
        grid=(batch, dil, nblk),
        in_specs=[pl.BlockSpec(memory_space=pltpu.SMEM), cur(0), prev(1), cur(1), prev(2), cur(2)],
        out_specs=(pl.BlockSpec(blk, lambda b, r, j: (b, j, r)),
                   pl.BlockSpec((1, BLK, LANES), lambda b, r, j: (b, j, r))),
        compiler_params=_params("parallel", "parallel", "parallel"),
        name="attn_a_prompt_g%d" % g,
    )(nslopes, view, view, view, view, view)
    return o.reshape(batch * seq, A_WIDTH), lse.reshape(batch * seq, LANES)


def _swap_halves(x):
    return pltpu.roll(x, LANES // 2, axis=1)


def _attn_b_kernel(nslope_ref, sink_ref, q_ref, kp_ref, kc_ref, vp_ref, vc_ref, o_ref):
    dist, valid = _band_dist_valid(B_WINDOW, pl.program_id(1) == 0)
    scale = B_HEAD_DIM ** -0.5
    lane_half = _div(lax.broadcasted_iota(jnp.int32, (BLK, LANES), 1), B_HEAD_DIM)
    for pair in range(B_KV_HEADS // 2):
        ps = slice(pair * LANES, (pair + 1) * LANES)
        k = jnp.concatenate([kp_ref[0, :, ps], kc_ref[0, :, ps]], axis=0).astype(_bf16)
        v = jnp.concatenate([vp_ref[0, :, ps], vc_ref[0, :, ps]], axis=0).astype(_bf16)
        for half in range(2):
            hk = pair * 2 + half
            qs = []
            for g in range(B_GROUP):
                hq = hk * B_GROUP + g
                blk = hq // 2
                qb = q_ref[0, :, blk * LANES:(blk + 1) * LANES]
                if hq % 2 != half:
                    qb = _swap_halves(qb)
                qs.append(jnp.where(lane_half == half, qb, 0.0).astype(_bf16))
            s_all = lax.dot_general(jnp.concatenate(qs, axis=0), k, _NT, preferred_element_type=_f32)
            ps_ = []
            ls = []
            for g in range(B_GROUP):
                hq = hk * B_GROUP + g
                s = s_all[g * BLK:(g + 1) * BLK] * scale
                s = jnp.where(valid, s + nslope_ref[hq] * dist, NEG_INF)
                sink = sink_ref[hq]
                m = jnp.maximum(jnp.max(s, axis=-1, keepdims=True), sink)
                p = jnp.exp(s - m)
                ls.append(jnp.sum(p, axis=-1, keepdims=True) + jnp.exp(sink - m))
                ps_.append(p.astype(_bf16))
            r_all = jnp.dot(jnp.concatenate(ps_, axis=0), v, preferred_element_type=_f32)
            for g2 in range(B_GROUP // 2):
                halves = []
                for sub in range(2):
                    g = g2 * 2 + sub
                    r = r_all[g * BLK:(g + 1) * BLK] / ls[g]
                    halves.append(r if sub == half else _swap_halves(r))
                blk = (hk * B_GROUP) // 2 + g2
                o_ref[0, :, blk * LANES:(blk + 1) * LANES] = jnp.where(lane_half == 0, halves[0], halves[1])


def _attn_b_prompt(proj, nslopes, sinks, batch, seq):
    nblk = seq // BLK
    view = proj.reshape(batch, seq, proj.shape[1])
    kcol = B_Q_WIDTH // B_KV_HALF
    kv = (1, BLK, B_KV_HALF)

    def cur(c):
        return pl.BlockSpec(kv, lambda b, j: (b, j, c))

    def prev(c):
        return pl.BlockSpec(kv, lambda b, j: (b, jnp.maximum(j - 1, 0), c))

    smem = pl.BlockSpec(memory_space=pltpu.SMEM)
    o = pl.pallas_call(
        _attn_b_kernel,
        out_shape=jax.ShapeDtypeStruct((batch, seq, B_Q_WIDTH), _f32),
        grid=(batch, nblk),
        in_specs=[smem, smem, pl.BlockSpec((1, BLK, B_Q_WIDTH), lambda b, j: (b, j, 0)),
                  prev(kcol), cur(kcol), prev(kcol + 1), cur(kcol + 1)],
        out_specs=pl.BlockSpec((1, BLK, B_Q_WIDTH), lambda b, j: (b, j, 0)),
        compiler_params=_params("parallel", "parallel"),
        name="attn_b_prompt",
    )(nslopes, sinks, view, view, view, view, view)
    return o.reshape(batch * seq, B_Q_WIDTH)


def _pad_rows(x, rows):
    return jnp.concatenate([x, jnp.zeros((rows - x.shape[0], x.shape[1]), x.dtype)], axis=0)


def _sample_bias(nslope, t, key_row, past, window, dil, pad_from):
    diff = past + t - key_row
    v = jnp.minimum(jnp.minimum(diff, window - diff), -_rem(diff, dil))
    if pad_from is not None:
        col = lax.broadcasted_iota(jnp.int32, diff.shape, 1)
        v = jnp.minimum(v, pad_from - 1 - col)
    return nslope * _div(diff, dil).astype(_f32), v >= 0


def _attn_a_sample_kernel(nslope_ref, qkv_ref, c0_ref, c1_ref, c2_ref, o_ref, *, steps, pasts):
    rows = A_HEADS * steps
    scale = A_HEAD_DIM ** -0.5
    row_id = lax.broadcasted_iota(jnp.int32, (rows, A_WIDTH), 0)
    lane_id = lax.broadcasted_iota(jnp.int32, (rows, A_WIDTH), 1)
    own_head = _div(row_id, steps) == _div(lane_id, A_HEAD_DIM)
    outs, lses = [], []
    for g, (window, dil) in enumerate(A_GROUPS):
        past = pasts[g]
        base = g * 3 * A_WIDTH
        q = qkv_ref[0, :, base:base + A_WIDTH]
        qbd = jnp.where(own_head, jnp.tile(q, (A_HEADS, 1)), 0.0).astype(_bf16)
        k_new = _pad_rows(qkv_ref[0, :, base + A_WIDTH:base + 2 * A_WIDTH], LANES).astype(_bf16)
        v_new = _pad_rows(qkv_ref[0, :, base + 2 * A_WIDTH:base + 3 * A_WIDTH], LANES).astype(_bf16)
        if g == 0:
            kv = c0_ref[0]
        elif g == 1:
            kv = c1_ref[0]
        else:
            kv = c2_ref[0].reshape(-1, 2 * A_WIDTH)
        n_keys = kv.shape[0]
        k_c = kv[:, :A_WIDTH].astype(_bf16)
        v_c = kv[:, A_WIDTH:].astype(_bf16)
        nslope = jnp.tile(nslope_ref[g], (1, n_keys // LANES))
        t = _rem(lax.broadcasted_iota(jnp.int32, (rows, n_keys), 0), steps)
        col = lax.broadcasted_iota(jnp.int32, (rows, n_keys), 1)
        if g == 2:
            key_row = _div(col, steps) * dil + _rem(col, steps)
        else:
            key_row = col
        bias_c, ok_c = _sample_bias(nslope, t, key_row, past, window, dil, None)
        t_n = _rem(lax.broadcasted_iota(jnp.int32, (rows, LANES), 0), steps)
        col_n = lax.broadcasted_iota(jnp.int32, (rows, LANES), 1)
        bias_n, ok_n = _sample_bias(nslope_ref[g], t_n, past + col_n, past, window, dil, steps)
        s_c = lax.dot_general(qbd, k_c, _NT, preferred_element_type=_f32) * scale
        s_n = lax.dot_general(qbd, k_new, _NT, preferred_element_type=_f32) * scale
        s_c = jnp.where(ok_c, s_c + bias_c, NEG_INF)
        s_n = jnp.where(ok_n, s_n + bias_n, NEG_INF)
        m = jnp.maximum(jnp.max(s_c, axis=-1, keepdims=True), jnp.max(s_n, axis=-1, keepdims=True))
        p_c = jnp.exp(s_c - m)
        p_n = jnp.exp(s_n - m)
        l = jnp.sum(p_c, axis=-1, keepdims=True) + jnp.sum(p_n, axis=-1, keepdims=True)
        o = (jnp.dot(p_c.astype(_bf16), v_c, preferred_element_type=_f32)
             + jnp.dot(p_n.astype(_bf16), v_new, preferred_element_type=_f32))
        outs.append(o / l)
        lses.append(m + jnp.log(l))
    m = jnp.maximum(jnp.maximum(lses[0], lses[1]), lses[2])
    es = [jnp.exp(x - m) for x in lses]
    inv = 1.0 / (es[0] + es[1] + es[2])
    comb = (es[0] * inv) * outs[0] + (es[1] * inv) * outs[1] + (es[2] * inv) * outs[2]
    comb = jnp.where(own_head, comb, 0.0)
    acc = comb[0:steps]
    for h in range(1, A_HEADS):
        acc = acc + comb[h * steps:(h + 1) * steps]
    o_ref[0] = acc


def _attn_a_sample(qkv, caches, nslope_rows, nb, steps):
    c0, c1, c2 = caches
    dil2 = A_GROUPS[2][1]
    assert c2.shape[1] % dil2 == 0 and steps <= dil2
    periods = c2.shape[1] // dil2
    c2v = c2.reshape(nb, periods, dil2, 2 * A_WIDTH)
    n_q = qkv.shape[1]
    rows = A_HEADS * steps
    pasts = tuple(c.shape[1] for c in caches)
    o = pl.pallas_call(
        functools.partial(_attn_a_sample_kernel, steps=steps, pasts=pasts),
        out_shape=jax.ShapeDtypeStruct((nb, steps, A_WIDTH), _f32),
        grid=(nb,),
        in_specs=[
            pl.BlockSpec((A_N_GROUPS, rows, LANES), lambda n: (0, 0, 0)),
            pl.BlockSpec((1, steps, n_q), lambda n: (n, 0, 0)),
            pl.BlockSpec((1,) + c0.shape[1:], lambda n: (n, 0, 0)),
            pl.BlockSpec((1,) + c1.shape[1:], lambda n: (n, 0, 0)),
            pl.BlockSpec((1, periods, steps, 2 * A_WIDTH), lambda n: (n, 0, 0, 0)),
        ],
        out_specs=pl.BlockSpec((1, steps, A_WIDTH), lambda n: (n, 0, 0)),
        compiler_params=_params("parallel"),
        name="attn_a_sample",
    )(nslope_rows, qkv.reshape(nb, steps, n_q), c0, c1, c2v)
    return o.reshape(nb * steps, A_WIDTH)


def _attn_b_sample_kernel(nslope_ref, sink_ref, proj_ref, c_ref, o_ref, *, steps):
    scale = B_HEAD_DIM ** -0.5
    rows = B_Q_HEADS * steps
    lane_half = _div(lax.broadcasted_iota(jnp.int32, (steps, LANES), 1), B_HEAD_DIM)
    zeros = jnp.zeros((steps, LANES), _f32)
    pieces = []
    for hq in range(B_Q_HEADS):
        hk = hq // B_GROUP
        qb = proj_ref[0, :, (hq // 2) * LANES:(hq // 2 + 1) * LANES]
        if hq % 2 != hk % 2:
            qb = _swap_halves(qb)
        qb = jnp.where(lane_half == hk % 2, qb, 0.0)
        pieces.append(jnp.concatenate([qb, zeros] if hk // 2 == 0 else [zeros, qb], axis=1))
    qbd = jnp.concatenate(pieces, axis=0).astype(_bf16)
    k_c = c_ref[0, :, 0:B_KV_HALF].astype(_bf16)
    v_c = c_ref[0, :, B_KV_HALF:2 * B_KV_HALF].astype(_bf16)
    k_new = _pad_rows(proj_ref[0, :, B_Q_WIDTH:B_Q_WIDTH + B_KV_HALF], LANES).astype(_bf16)
    v_new = _pad_rows(proj_ref[0, :, B_Q_WIDTH + B_KV_HALF:B_Q_WIDTH + 2 * B_KV_HALF], LANES).astype(_bf16)
    n_keys = c_ref.shape[1]
    nslope = nslope_ref[...]
    sink = sink_ref[:, 0:1]
    t = _rem(lax.broadcasted_iota(jnp.int32, (rows, n_keys), 0), steps)
    col = lax.broadcasted_iota(jnp.int32, (rows, n_keys), 1)
    bias_c, ok_c = _sample_bias(jnp.tile(nslope, (1, n_keys // LANES)), t, col, n_keys, B_WINDOW, 1, None)
    t_n = _rem(lax.broadcasted_iota(jnp.int32, (rows, LANES), 0), steps)
    col_n = lax.broadcasted_iota(jnp.int32, (rows, LANES), 1)
    bias_n, ok_n = _sample_bias(nslope, t_n, n_keys + col_n, n_keys, B_WINDOW, 1, steps)
    s_c = lax.dot_general(qbd, k_c, _NT, preferred_element_type=_f32) * scale
    s_n = lax.dot_general(qbd, k_new, _NT, preferred_element_type=_f32) * scale
    s_c = jnp.where(ok_c, s_c + bias_c, NEG_INF)
    s_n = jnp.where(ok_n, s_n + bias_n, NEG_INF)
    m = jnp.maximum(jnp.max(s_c, axis=-1, keepdims=True), jnp.max(s_n, axis=-1, keepdims=True))
    m = jnp.maximum(m, sink)
    p_c = jnp.exp(s_c - m)
    p_n = jnp.exp(s_n - m)
    l = jnp.sum(p_c, axis=-1, keepdims=True) + jnp.sum(p_n, axis=-1, keepdims=True) + jnp.exp(sink - m)
    o = (jnp.dot(p_c.astype(_bf16), v_c, preferred_element_type=_f32)
         + jnp.dot(p_n.astype(_bf16), v_new, preferred_element_type=_f32)) / l
    for blk in range(B_Q_HEADS // 2):
        halves = []
        for sub in range(2):
            hq = blk * 2 + sub
            hk = hq // B_GROUP
            r = o[hq * steps:(hq + 1) * steps, (hk // 2) * LANES:(hk // 2 + 1) * LANES]
            halves.append(r if sub == hk % 2 else _swap_halves(r))
        o_ref[0, :, blk * LANES:(blk + 1) * LANES] = jnp.where(lane_half == 0, halves[0], halves[1])


def _attn_b_sample(proj, cache, nslope_rows, sink_rows, nb, steps):
    n_p = proj.shape[1]
    rows = B_Q_HEADS * steps
    o = pl.pallas_call(
        functools.partial(_attn_b_sample_kernel, steps=steps),
        out_shape=jax.ShapeDtypeStruct((nb, steps, B_Q_WIDTH), _f32),
        grid=(nb,),
        in_specs=[
            pl.BlockSpec((rows, LANES), lambda n: (0, 0)),
            pl.BlockSpec((rows, LANES), lambda n: (0, 0)),
            pl.BlockSpec((1, steps, n_p), lambda n: (n, 0, 0)),
            pl.BlockSpec((1,) + cache.shape[1:], lambda n: (n, 0, 0)),
        ],
        out_specs=pl.BlockSpec((1, steps, B_Q_WIDTH), lambda n: (n, 0, 0)),
        compiler_params=_params("parallel"),
        name="attn_b_sample",
    )(nslope_rows, sink_rows, proj.reshape(nb, steps, n_p), cache)
    return o.reshape(nb * steps, B_Q_WIDTH)


def _alibi_slopes(n):
    return 2.0 ** (-8.0 * jnp.arange(1, n + 1, dtype=_f32) / n)


def _rows_lanes(v, steps):
    return jnp.broadcast_to(jnp.repeat(v, steps)[:, None], (v.shape[0] * steps, LANES))


def kernel(x_prompt, x_sample, cache_a_w128, cache_a_w512, cache_a_w2048, cache_b, c_prompt, c_sample,
           norm_mix_g, norm_ffn_g, w_ada, b_ada, w_a_in, w_a_out, w_b_in, b_sinks, w_b_out,
           w_ffn_gate, w_ffn_up, w_ffn_down, norm_final_g):
    batch, seq, d = x_prompt.shape
    nb, steps, _ = x_sample.shape
    depth = w_ada.shape[0]
    a_caches = (cache_a_w128, cache_a_w512, cache_a_w2048)

    n_c = batch + nb
    pad = -n_c % SUBLANES
    c_all = jnp.concatenate([c_prompt, c_sample, jnp.zeros((pad, d), _f32)], axis=0)
    mod = _modulation(c_all, w_ada, b_ada)
    mod_p = mod[:, :batch].reshape(depth, batch, 1, 6 * d)
    mod_s = jnp.repeat(mod[:, batch:n_c], steps, axis=1)

    a_slopes = _alibi_slopes(A_N_GROUPS * A_HEADS).reshape(A_N_GROUPS, A_HEADS)
    b_slopes = _alibi_slopes(B_Q_HEADS)
    a_nslope = jnp.stack([-(a_slopes[g] * dil) for g, (_, dil) in enumerate(A_GROUPS)])
    b_nslope = -(b_slopes * 1.0)

    xp = x_prompt.reshape(batch * seq, d)
    xs = x_sample.reshape(nb * steps, d)
    tm_p = min(512, seq)
    tm_s = min(512, nb * steps)
    tf = w_ffn_gate.shape[2] // 2
    pa_rows = [[] for _ in A_GROUPS]
    sa_rows = [[] for _ in A_GROUPS]
    pb_rows, sb_rows = [], []

    for i in range(depth):
        li = i // 2
        last = i == depth - 1
        if i % 2 == 0:
            w_in = w_a_in[li].astype(_bf16)
            w_out = w_a_out[li].astype(_bf16)
            qkv = _norm_proj(xp, mod_p[i], norm_mix_g[i], w_in, rows_per_batch=seq, tm=tm_p, tn=1024,
                             name="a_qkv_prompt")
            outs, lses = [], []
            for g in range(A_N_GROUPS):
                o, lse = _attn_a_prompt(qkv, a_nslope[g], g, batch, seq)
                outs.append(o)
                lses.append(lse)
            xp = _a_out_proj(outs, lses, w_out, xp, mod_p[i], rows_per_batch=seq, tm=tm_p)
            q6 = qkv.reshape(batch, seq, A_N_GROUPS, 3, A_HEADS, A_HEAD_DIM)
            for g, (window, _) in enumerate(A_GROUPS):
                pa_rows[g].append(q6[:, seq - min(window, seq):, g, 1:3])
            qkv = _norm_proj(xs, mod_s[i], norm_mix_g[i], w_in, rows_per_batch=steps, tm=tm_s, tn=1024,
                             name="a_qkv_sample")
            caches = [c[li].reshape(nb, c.shape[2], 2 * A_WIDTH) for c in a_caches]
            nslope_rows = jnp.stack([_rows_lanes(a_nslope[g], steps) for g in range(A_N_GROUPS)])
            o = _attn_a_sample(qkv, caches, nslope_rows, nb, steps)
            xs = _out_proj(o, w_out, xs, mod_s[i], rows_per_batch=steps, tm=tm_s, name="a_out_proj_sample")
            q6 = qkv.reshape(nb, steps, A_N_GROUPS, 3, A_HEADS, A_HEAD_DIM)
            for g in range(A_N_GROUPS):
                sa_rows[g].append(q6[:, :, g, 1:3])
        else:
            w_in = w_b_in[li].astype(_bf16)
            w_out = w_b_out[li].astype(_bf16)
            sinks = b_sinks[li].reshape(B_Q_HEADS).astype(_f32)
            n_proj = w_in.shape[1]
            proj = _norm_proj(xp, mod_p[i], norm_mix_g[i], w_in, rows_per_batch=seq, tm=tm_p, tn=n_proj // 3,
                              name="b_proj_prompt")
            o = _attn_b_prompt(proj, b_nslope, sinks, batch, seq)
            xp = _out_proj(o, w_out, xp, mod_p[i], rows_per_batch=seq, tm=tm_p, name="b_out_proj_prompt")
            keep = min(B_WINDOW, seq)
            pb_rows.append(proj.reshape(batch, seq, n_proj)[:, seq - keep:, B_Q_WIDTH:]
                           .reshape(batch, keep, 2, B_KV_HEADS, B_HEAD_DIM))
            proj = _norm_proj(xs, mod_s[i], norm_mix_g[i], w_in, rows_per_batch=steps, tm=tm_s, tn=n_proj // 3,
                              name="b_proj_sample")
            cache = cache_b[li].reshape(nb, cache_b.shape[2], 2 * B_KV_HALF)
            o = _attn_b_sample(proj, cache, _rows_lanes(b_nslope, steps), _rows_lanes(sinks, steps), nb, steps)
            xs = _out_proj(o, w_out, xs, mod_s[i], rows_per_batch=steps, tm=tm_s, name="b_out_proj_sample")
            sb_rows.append(proj[:, B_Q_WIDTH:].reshape(nb, steps, 2, B_KV_HEADS, B_HEAD_DIM))
        wg = w_ffn_gate[i].astype(_bf16)
        wu = w_ffn_up[i].astype(_bf16)
        wd = w_ffn_down[i].astype(_bf16)
        xp = _ffn(xp, mod_p[i], norm_ffn_g[i], wg, wu, wd, norm_final_g, rows_per_batch=seq, tm=tm_p, tf=tf,
                  final_norm=last, name="ffn_prompt")
        xs = _ffn(xs, mod_s[i], norm_ffn_g[i], wg, wu, wd, norm_final_g, rows_per_batch=steps, tm=tm_s, tf=tf,
                  final_norm=last, name="ffn_sample")

    return (xp.reshape(batch, seq, d), xs.reshape(nb, steps, d),
            jnp.stack(pa_rows[0]), jnp.stack(pa_rows[1]), jnp.stack(pa_rows[2]), jnp.stack(pb_rows),
            jnp.stack(sa_rows[0]), jnp.stack(sa_rows[1]), jnp.stack(sa_rows[2]), jnp.stack(sb_rows))
```
